```python
import math
import jax, jax.numpy as jnp
from jax import lax
import numpy as np

D_MODEL = 2048
BATCH = 8
SEQ = 2048
DEPTH = 1

EPS = 1e-6
D_MIX = D_MODEL
RET_HEADS = 8
RET_DV = D_MIX // (2 * RET_HEADS)
RET_DK = RET_DV // 2
RET_CHUNK = 128
RET_W = RET_HEADS * RET_DV
NSA_HEADS = 8
NSA_GROUPS = 2
NSA_HPG = NSA_HEADS // NSA_GROUPS
NSA_DH = D_MIX // (2 * NSA_HEADS)
NSA_W = NSA_HEADS * NSA_DH
KV_W = NSA_GROUPS * NSA_DH
CMP_LEN = 32
CMP_STRIDE = 16
CMP_HIDDEN = 2 * NSA_DH
SEL_BLOCK = 64
SEL_TOPK = 16
SEL_QCHUNK = 32
WINDOW = 512
WIN_QBLOCK = 128
N_BUCKETS = 32
MAX_DISTANCE = 128
PEER_HEADS = 8
PEER_NKEYS = 128
PEER_N = PEER_NKEYS * PEER_NKEYS
PEER_DQ = 256
PEER_TOPK = 16
PEER_TOKCHUNK = 128
SPLITS = (RET_HEADS * RET_DK, RET_HEADS * RET_DK, RET_W, RET_W, NSA_W, 6 * KV_W, 3 * NSA_HEADS)
IN_COLS = 2 * RET_HEADS * RET_DK + 2 * RET_W + NSA_W + 6 * KV_W + 3 * NSA_HEADS

kernel_name = 'hybrid_retnet_nsa_peer'


def rms_normalize(x):
    xf = x.astype(jnp.float32)
    return (xf * lax.rsqrt(jnp.mean(xf * xf, axis=-1, keepdims=True) + EPS)).astype(x.dtype)


def rmsnorm(x, g):
    return rms_normalize(x) * g


def masked_softmax(s, mask):
    s = jnp.where(mask, s.astype(jnp.float32), -1e30)
    m = jnp.max(s, axis=-1, keepdims=True)
    e = jnp.exp(s - m) * mask
    return e / jnp.maximum(jnp.sum(e, axis=-1, keepdims=True), 1e-30)


def t5_bucket(dist):
    dist = jnp.maximum(dist, 0)
    max_exact = N_BUCKETS // 2
    log_ratio = jnp.log(jnp.maximum(dist, 1).astype(jnp.float32) / max_exact) / math.log(MAX_DISTANCE / max_exact)
    large = jnp.minimum(max_exact + (log_ratio * (N_BUCKETS - max_exact)).astype(jnp.int32), N_BUCKETS - 1)
    return jnp.where(dist < max_exact, dist, large)


def rotary(x, pos):
    half = x.shape[-1] // 2
    inv = 1.0 / (10000.0 ** (jnp.arange(half, dtype=jnp.float32) / half))
    ang = pos.astype(jnp.float32)[:, None] * inv[None, :]
    cos = jnp.cos(ang)[None, :, None, :].astype(x.dtype)
    sin = jnp.sin(ang)[None, :, None, :].astype(x.dtype)
    x1, x2 = x[..., :half], x[..., half:]
    return jnp.concatenate([x1 * cos - x2 * sin, x1 * sin + x2 * cos], axis=-1)


def retention_chunkwise(q, k, v):
    B, T, H, dk = q.shape
    dv = v.shape[-1]
    C = RET_CHUNK
    nc = T // C
    q = q.astype(jnp.float32) * dk ** -0.5
    k = k.astype(jnp.float32)
    v = v.astype(jnp.float32)
    log_g = jnp.log(1.0 - 2.0 ** (-5.0 - jnp.arange(H, dtype=jnp.float32)))
    qc = q.reshape(B, nc, C, H, dk)
    kc = k.reshape(B, nc, C, H, dk)
    vc = v.reshape(B, nc, C, H, dv)
    idx = jnp.arange(C, dtype=jnp.float32)
    diff = idx[:, None] - idx[None, :]
    dmask = jnp.where(diff >= 0, jnp.exp(log_g[:, None, None] * jnp.maximum(diff, 0.0)), 0.0)
    s = jnp.einsum('bcnhd,bcmhd->bchnm', qc, kc) * dmask
    intra = jnp.einsum('bchnm,bcmhe->bcnhe', s, vc)
    zeta = jnp.exp(log_g[:, None] * (C - 1 - idx)[None, :])
    xi = jnp.exp(log_g[:, None] * (idx + 1.0)[None, :])
    kv = jnp.einsum('bcmhd,hm,bcmhe->bchde', kc, zeta, vc)
    decay_c = jnp.exp(log_g * C)[None, :, None, None]

    def step(state, kv_c):
        return state * decay_c + kv_c, state

    _, r_prev = lax.scan(step, jnp.zeros((B, H, dk, dv), jnp.float32), jnp.moveaxis(kv, 1, 0))
    r_prev = jnp.moveaxis(r_prev, 0, 1)
    cross = jnp.einsum('bcnhd,bchde,hn->bcnhe', qc, r_prev, xi)
    return (intra + cross).reshape(B, T, H, dv)


def retention_group(rq, rk, rv, rg, gn_g, pos):
    B, T, _ = rq.shape
    q = rotary(rq.reshape(B, T, RET_HEADS, RET_DK), pos)
    k = rotary(rk.reshape(B, T, RET_HEADS, RET_DK), pos)
    v = rv.reshape(B, T, RET_HEADS, RET_DV)
    y = retention_chunkwise(q, k, v)
    mu = jnp.mean(y, axis=-1, keepdims=True)
    var = jnp.mean(jnp.square(y - mu), axis=-1, keepdims=True)
    y = ((y - mu) * lax.rsqrt(var + EPS)).reshape(B, T, RET_W).astype(rg.dtype) * gn_g
    return jax.nn.silu(rg) * y


def compress(a, pe, w1, w2):
    B, T, G, dh = a.shape
    nc = (T - CMP_LEN) // CMP_STRIDE + 1
    idx = jnp.arange(nc)[:, None] * CMP_STRIDE + jnp.arange(CMP_LEN)[None, :]
    blk = a[:, idx] + pe[:, None, :]
    blk = jnp.moveaxis(blk, 3, 2).reshape(B, nc, G, CMP_LEN * dh)
    return jax.nn.gelu(blk @ w1) @ w2


def cmp_attention(q, kc, vc, rb):
    T = q.shape[1]
    nc = kc.shape[1]
    s = jnp.einsum('btghd,bcgd->bghtc', q, kc)
    end = jnp.arange(nc) * CMP_STRIDE + CMP_LEN - 1
    dist = jnp.arange(T)[:, None] - end[None, :]
    bias = jnp.moveaxis(rb[t5_bucket(dist)], (2, 3), (0, 1))
    p = masked_softmax(s + bias, dist >= 0)
    o = jnp.einsum('bghtc,bcgd->btghd', p.astype(vc.dtype), vc)
    return o, p


def sel_attention(q, k, v, p_cmp, rb):
    B, T = q.shape[:2]
    nsel = T // SEL_BLOCK
    nc = p_cmp.shape[-1]
    topk = min(SEL_TOPK, nsel)
    cs = jnp.arange(nc) * CMP_STRIDE
    js = jnp.arange(nsel) * SEL_BLOCK
    overlap = jnp.clip(jnp.minimum(cs[:, None] + CMP_LEN, js[None, :] + SEL_BLOCK) - jnp.maximum(cs[:, None], js[None, :]), 0, None).astype(jnp.float32) / CMP_LEN
    imp = jnp.einsum('bghtc,cj->bgtj', p_cmp, overlap)
    t = jnp.arange(T)
    j = jnp.arange(nsel)
    cur = t // SEL_BLOCK
    causal = j[None, :] * SEL_BLOCK <= t[:, None]
    forced = (j[None, :] == 0) | (j[None, :] == cur[:, None]) | (j[None, :] == cur[:, None] - 1)
    imp = jnp.where(forced, 1e9, imp)
    imp = jnp.where(causal, imp, -1e9)
    vals, idx = lax.top_k(imp, topk)
    sel_ok = vals > -1e8
    kb = jnp.moveaxis(k.reshape(B, nsel, SEL_BLOCK, NSA_GROUPS, NSA_DH), 3, 1)
    vb = jnp.moveaxis(v.reshape(B, nsel, SEL_BLOCK, NSA_GROUPS, NSA_DH), 3, 1)
    nq = T // SEL_QCHUNK
    qs = jnp.moveaxis(q.reshape(B, nq, SEL_QCHUNK, NSA_GROUPS, NSA_HPG, NSA_DH), 1, 0)
    idx_s = jnp.moveaxis(idx.reshape(B, NSA_GROUPS, nq, SEL_QCHUNK, topk), 2, 0)
    ok_s = jnp.moveaxis(sel_ok.reshape(B, NSA_GROUPS, nq, SEL_QCHUNK, topk), 2, 0)
    bi = jnp.arange(B)[:, None, None, None]
    gi = jnp.arange(NSA_GROUPS)[None, :, None, None]
    gi5 = gi[..., None]
    sb = jnp.arange(SEL_BLOCK)
    n_keys = topk * SEL_BLOCK

    def chunk(args):
        qc, ic, okc, t0 = args
        kg = kb[bi, gi, ic]
        vg = vb[bi, gi, ic]
        tq = t0 + jnp.arange(SEL_QCHUNK)
        kpos = ic[..., None] * SEL_BLOCK + sb
        dist = tq[:, None, None] - kpos
        mask = okc[..., None] & (dist >= 0)
        bias = jnp.moveaxis(rb[t5_bucket(dist), gi5], -1, 2)
        s = jnp.einsum('btghd,bgtksd->bghtks', qc, kg) + bias
        p = masked_softmax(s.reshape(B, NSA_GROUPS, NSA_HPG, SEL_QCHUNK, n_keys), mask.reshape(B, NSA_GROUPS, 1, SEL_QCHUNK, n_keys))
        return jnp.einsum('bghtn,bgtnd->btghd', p.astype(vg.dtype), vg.reshape(B, NSA_GROUPS, SEL_QCHUNK, n_keys, NSA_DH))

    o = lax.map(chunk, (qs, idx_s, ok_s, jnp.arange(nq) * SEL_QCHUNK))
    return jnp.moveaxis(o, 0, 1).reshape(B, T, NSA_GROUPS, NSA_HPG, NSA_DH)


def win_attention(q, k, v, rb):
    B, T = q.shape[:2]
    nb = T // WIN_QBLOCK
    nwin = WINDOW // WIN_QBLOCK + 1
    K = nwin * WIN_QBLOCK
    widx = jnp.arange(nb)[:, None] + jnp.arange(nwin)[None, :]

    def band(a):
        ap = jnp.pad(a, ((0, 0), (WINDOW, 0), (0, 0), (0, 0))).reshape(B, nb + nwin - 1, WIN_QBLOCK, NSA_GROUPS, NSA_DH)
        return jnp.moveaxis(ap[:, widx].reshape(B, nb, K, NSA_GROUPS, NSA_DH), 1, 0)

    kw, vw = band(k), band(v)
    qb = jnp.moveaxis(q.reshape(B, nb, WIN_QBLOCK, NSA_GROUPS, NSA_HPG, NSA_DH), 1, 0)
    ki = jnp.arange(K)
    dist = WINDOW + jnp.arange(WIN_QBLOCK)[:, None] - ki[None, :]
    in_band = (dist >= 0) & (dist < WINDOW)
    bias = jnp.moveaxis(rb[t5_bucket(dist)], (2, 3), (0, 1))

    def block(args):
        qn, kn, vn, n = args
        kpos = n * WIN_QBLOCK - WINDOW + ki
        mask = in_band & (kpos >= 0)[None, :]
        s = jnp.einsum('bqghd,bkgd->bghqk', qn, kn) + bias
        p = masked_softmax(s, mask)
        return jnp.einsum('bghqk,bkgd->bqghd', p.astype(vn.dtype), vn)

    o = lax.map(block, (qb, kw, vw, jnp.arange(nb)))
    return jnp.moveaxis(o, 0, 1).reshape(B, T, NSA_GROUPS, NSA_HPG, NSA_DH)


def nsa_group(nq, nkv, ngate, rel_bias, cmp_pe, cmp_w1, cmp_w2, out_g):
    B, T, _ = nq.shape
    q = nq.reshape(B, T, NSA_GROUPS, NSA_HPG, NSA_DH) * NSA_DH ** -0.5
    kv = nkv.reshape(B, T, 6, NSA_GROUPS, NSA_DH)
    rb = rel_bias.reshape(N_BUCKETS, NSA_GROUPS, NSA_HPG)
    k_cmp = compress(kv[:, :, 0], cmp_pe[0], cmp_w1[0], cmp_w2[0])
    v_cmp = compress(kv[:, :, 1], cmp_pe[1], cmp_w1[1], cmp_w2[1])
    o_cmp, p_cmp = cmp_attention(q, k_cmp, v_cmp, rb)
    o_sel = sel_attention(q, kv[:, :, 2], kv[:, :, 3], p_cmp, rb)
    o_win = win_attention(q, kv[:, :, 4], kv[:, :, 5], rb)
    gate = jax.nn.sigmoid(ngate.reshape(B, T, NSA_GROUPS, NSA_HPG, 3))
    o = gate[..., 0:1] * o_cmp + gate[..., 1:2] * o_sel + gate[..., 2:3] * o_win
    return rms_normalize(o).reshape(B, T, NSA_W) * out_g


def peer_ffn(x, wq, subkeys, u, v):
    B, T, D = x.shape
    n = B * T
    xt = x.reshape(n, D)
    q = (xt @ wq).reshape(n, PEER_HEADS, 2, PEER_DQ // 2)
    s = jnp.einsum('nhpd,pkd->nhpk', q, subkeys)
    sv, si = lax.top_k(s, PEER_TOPK)
    cand = sv[:, :, 0, :, None] + sv[:, :, 1, None, :]
    cidx = si[:, :, 0, :, None] * PEER_NKEYS + si[:, :, 1, None, :]
    cs, ci = lax.top_k(cand.reshape(n, PEER_HEADS, PEER_TOPK * PEER_TOPK), PEER_TOPK)
    eidx = jnp.take_along_axis(cidx.reshape(n, PEER_HEADS, PEER_TOPK * PEER_TOPK), ci, axis=-1)
    gate = jax.nn.softmax(cs.astype(jnp.float32), axis=-1).astype(x.dtype)
    nch = n // PEER_TOKCHUNK

    def chunk(args):
        xc, ec, gc = args
        hid = jax.nn.gelu(jnp.einsum('cd,chkd->chk', xc, u[ec]))
        return jnp.einsum('chk,chkd->cd', gc * hid, v[ec])

    out = lax.map(chunk, (xt.reshape(nch, PEER_TOKCHUNK, D), eidx.reshape(nch, PEER_TOKCHUNK, PEER_HEADS, PEER_TOPK), gate.reshape(nch, PEER_TOKCHUNK, PEER_HEADS, PEER_TOPK)))
    return out.reshape(B, T, D)


def setup_inputs(seed: int = 0) -> dict:
    key = jax.random.key(seed)
    ks = jax.random.split(key, 16)
    nrm = jax.random.normal
    f32 = jnp.float32
    return {
        'x': nrm(ks[0], (BATCH, SEQ, D_MODEL), f32),
        'norm1_g': 1.0 + 0.02 * nrm(ks[1], (DEPTH, D_MODEL), f32),
        'w_in': nrm(ks[2], (DEPTH, D_MODEL, IN_COLS), f32) * D_MODEL ** -0.5,
        'rel_bias': 0.5 * nrm(ks[3], (N_BUCKETS, NSA_HEADS), f32),
        'ret_gn_g': 1.0 + 0.02 * nrm(ks[4], (DEPTH, RET_W), f32),
        'cmp_pe': 0.2 * nrm(ks[5], (DEPTH, 2, CMP_LEN, NSA_DH), f32),
        'cmp_w1': nrm(ks[6], (DEPTH, 2, CMP_LEN * NSA_DH, CMP_HIDDEN), f32) * (CMP_LEN * NSA_DH) ** -0.5,
        'cmp_w2': nrm(ks[7], (DEPTH, 2, CMP_HIDDEN, NSA_DH), f32) * CMP_HIDDEN ** -0.5,
        'nsa_out_g': 1.0 + 0.02 * nrm(ks[8], (DEPTH, NSA_W), f32),
        'w_out': nrm(ks[9], (DEPTH, D_MIX, D_MODEL), f32) * D_MIX ** -0.5,
        'norm2_g': 1.0 + 0.02 * nrm(ks[10], (DEPTH, D_MODEL), f32),
        'peer_wq': nrm(ks[11], (DEPTH, D_MODEL, PEER_HEADS * PEER_DQ), f32) * D_MODEL ** -0.5,
        'peer_subkeys': nrm(ks[12], (DEPTH, 2, PEER_NKEYS, PEER_DQ // 2), f32) * (PEER_DQ // 2) ** -0.5,
        'peer_u': nrm(ks[13], (DEPTH, PEER_N, D_MODEL), f32) * D_MODEL ** -0.5,
        'peer_v': 0.5 * nrm(ks[14], (DEPTH, PEER_N, D_MODEL), f32),
        'norm_f_g': 1.0 + 0.02 * nrm(ks[15], (D_MODEL,), f32),
    }


def reference(x, norm1_g, w_in, rel_bias, ret_gn_g, cmp_pe, cmp_w1, cmp_w2, nsa_out_g, w_out, norm2_g, peer_wq, peer_subkeys, peer_u, peer_v, norm_f_g):
    T = x.shape[1]
    pos = jnp.arange(T)
    offsets = [int(o) for o in np.cumsum(SPLITS)[:-1]]
    h = x
    for l in range(DEPTH):
        xn = rmsnorm(h, norm1_g[l])
        proj = xn @ w_in[l]
        rq, rk, rv, rg, nq, nkv, ngate = jnp.split(proj, offsets, axis=-1)
        ret = retention_group(rq, rk, rv, rg, ret_gn_g[l], pos)
        nsa = nsa_group(nq, nkv, ngate, rel_bias, cmp_pe[l], cmp_w1[l], cmp_w2[l], nsa_out_g[l])
        h = h + jnp.concatenate([ret, nsa], axis=-1) @ w_out[l]
        h = h + peer_ffn(rmsnorm(h, norm2_g[l]), peer_wq[l], peer_subkeys[l], peer_u[l], peer_v[l])
    return rmsnorm(h, norm_f_g)
```

```python
import functools
import math

import numpy as np
import jax
import jax.numpy as jnp
from jax import lax
from jax.experimental import pallas as pl
from jax.experimental.pallas import tpu as pltpu

F32 = jnp.float32
BF16 = jnp.bfloat16

LANES = 128
VMEM_LIMIT_BYTES = 56 * 1024 * 1024

D_MODEL = 2048
EPS = 1e-6
D_MIX = D_MODEL
RET_HEADS = 8
RET_DV = D_MIX // (2 * RET_HEADS)
RET_DK = RET_DV // 2
RET_CHUNK = 128
RET_W = RET_HEADS * RET_DV
NSA_HEADS = 8
NSA_GROUPS = 2
NSA_HPG = NSA_HEADS // NSA_GROUPS
NSA_DH = D_MIX // (2 * NSA_HEADS)
NSA_W = NSA_HEADS * NSA_DH
KV_W = NSA_GROUPS * NSA_DH
CMP_LEN = 32
CMP_STRIDE = 16
CMP_HIDDEN = 2 * NSA_DH
SEL_BLOCK = 64
SEL_TOPK = 16
WINDOW = 512
N_BUCKETS = 32
MAX_DISTANCE = 128
PEER_HEADS = 8
PEER_NKEYS = 128
PEER_DQ = 256
PEER_TOPK = 16
PEER_EXPERTS_PER_TOKEN = PEER_HEADS * PEER_TOPK

QK_OFF = 0
RV_OFF = QK_OFF + RET_HEADS * LANES
RG_OFF = RV_OFF + RET_W
NQ_OFF = RG_OFF + RET_W
NKV_OFF = NQ_OFF + NSA_W
NGATE_OFF = NKV_OFF + 6 * KV_W
PROJ_W = 6144
Q_TILE = 128


def _proj_column_source():
    src = np.full((PROJ_W,), -1, np.int64)
    half = RET_DK // 2
    rq0, rk0 = 0, RET_HEADS * RET_DK
    rv0 = 2 * RET_HEADS * RET_DK
    rg0 = rv0 + RET_W
    nq0 = rg0 + RET_W
    nkv0 = nq0 + NSA_W
    ng0 = nkv0 + 6 * KV_W
    for h in range(RET_HEADS):
        base = QK_OFF + h * LANES
        src[base:base + RET_DK] = rq0 + h * RET_DK + np.arange(RET_DK)
        src[base + RET_DK:base + 2 * RET_DK] = rk0 + h * RET_DK + np.arange(RET_DK)
    del half
    src[RV_OFF:RV_OFF + RET_W] = rv0 + np.arange(RET_W)
    src[RG_OFF:RG_OFF + RET_W] = rg0 + np.arange(RET_W)
    src[NQ_OFF:NQ_OFF + NSA_W] = nq0 + np.arange(NSA_W)
    src[NKV_OFF:NKV_OFF + 6 * KV_W] = nkv0 + np.arange(6 * KV_W)
    per_group = NSA_HPG * 3
    for g in range(NSA_GROUPS):
        src[NGATE_OFF + g * LANES:NGATE_OFF + g * LANES + per_group] = ng0 + g * per_group + np.arange(per_group)
    return src


def _cparams(sem, vmem=VMEM_LIMIT_BYTES):
    return pltpu.CompilerParams(dimension_semantics=sem, vmem_limit_bytes=vmem)


def _nt_dot(a, b, **kw):
    return lax.dot_general(a, b, (((1,), (1,)), ((), ())), preferred_element_type=F32, **kw)


def _tn_dot(a, b):
    return lax.dot_general(a, b, (((0,), (0,)), ((), ())), preferred_element_type=F32)


def _t5_bucket(dist):
    dist = jnp.maximum(dist, 0)
    max_exact = N_BUCKETS // 2
    log_ratio = jnp.log(jnp.maximum(dist, 1).astype(F32) / max_exact) / math.log(MAX_DISTANCE / max_exact)
    large = jnp.minimum(max_exact + (log_ratio * (N_BUCKETS - max_exact)).astype(jnp.int32), N_BUCKETS - 1)
    return jnp.where(dist < max_exact, dist, large)


def _inproj_kernel(x_ref, g_ref, w_ref, o_ref, xn_ref):
    @pl.when(pl.program_id(1) == 0)
    def _():
        x = x_ref[...]
        ms = jnp.mean(x * x, axis=-1, keepdims=True)
        xn_ref[...] = (x * lax.rsqrt(ms + EPS) * g_ref[...]).astype(BF16)

    o_ref[...] = jnp.dot(xn_ref[...], w_ref[...], preferred_element_type=F32)


def _in_projection(x2, g1, w_perm, tm=1024, tn=1024):
    n = x2.shape[0]
    tm = min(tm, n)
    return pl.pallas_call(
        _inproj_kernel,
        grid=(n // tm, PROJ_W // tn),
        in_specs=[
            pl.BlockSpec((tm, D_MODEL), lambda i, j: (i, 0)),
            pl.BlockSpec((1, D_MODEL), lambda i, j: (0, 0)),
            pl.BlockSpec((D_MODEL, tn), lambda i, j: (0, j)),
        ],
        out_specs=pl.BlockSpec((tm, tn), lambda i, j: (i, j)),
        out_shape=jax.ShapeDtypeStruct((n, PROJ_W), F32),
        scratch_shapes=[pltpu.VMEM((tm, D_MODEL), BF16)],
        compiler_params=_cparams(("parallel", "arbitrary")),
        name="in_projection",
    )(x2, g1, w_perm)


def _retention_kernel(qk_ref, v_ref, g_ref, cos_ref, sin_ref, dm_ref, xi_ref, ze_ref, de_ref, gn_ref, o_ref):
    seq = qk_ref.shape[0]
    lane = lax.broadcasted_iota(jnp.int32, (RET_CHUNK, LANES), 1)
    first_half = (lane & (RET_DK - 1)) < (RET_DK // 2)
    q_scale = jnp.where(lane < RET_DK, RET_DK ** -0.5, 0.0).astype(F32)
    dm = dm_ref[0]
    xi = xi_ref[0]
    ze = ze_ref[0]
    de = de_ref[0]
    gn = gn_ref[...]

    def body(c, state):
        r0 = pl.multiple_of(c * RET_CHUNK, RET_CHUNK)
        rows = pl.ds(r0, RET_CHUNK)
        z = qk_ref[rows, :]
        partner = jnp.where(first_half, pltpu.roll(z, LANES - RET_DK // 2, 1), pltpu.roll(z, RET_DK // 2, 1))
        rot = z * cos_ref[rows, :] + partner * sin_ref[rows, :]
        qm = rot * q_scale
        kk = pltpu.roll(rot, RET_DK, 1)
        v = v_ref[rows, :]
        s = _nt_dot(qm, kk) * dm
        y = jnp.dot(s, v, preferred_element_type=F32) + jnp.dot(qm * xi, state, preferred_element_type=F32)
        kv = _tn_dot(kk * ze, v)
        mu = jnp.mean(y, axis=-1, keepdims=True)
        yc = y - mu
        var = jnp.mean(yc * yc, axis=-1, keepdims=True)
        yn = yc * lax.rsqrt(var + EPS) * gn
        gate = g_ref[rows, :]
        o_ref[rows, :] = (gate * jax.nn.sigmoid(gate) * yn).astype(o_ref.dtype)
        return state * de + kv

    lax.fori_loop(0, seq // RET_CHUNK, body, jnp.zeros((LANES, RET_DV), F32))


def _retention_tables(seq):
    half = RET_DK // 2
    pos = jnp.arange(seq, dtype=F32)
    inv = 1.0 / (10000.0 ** (jnp.arange(half, dtype=F32) / half))
    ang = pos[:, None] * inv[None, :]
    lane = np.arange(LANES)
    cos = jnp.cos(ang)[:, lane % half]
    sign = np.where((lane % (2 * half)) < half, -1.0, 1.0).astype(np.float32)
    sin = jnp.sin(ang)[:, lane % half] * sign[None, :]
    log_g = jnp.log(1.0 - 2.0 ** (-5.0 - jnp.arange(RET_HEADS, dtype=F32)))
    idx = jnp.arange(RET_CHUNK, dtype=F32)
    diff = idx[:, None] - idx[None, :]
    dmask = jnp.where(diff >= 0, jnp.exp(log_g[:, None, None] * jnp.maximum(diff, 0.0)), 0.0)
    zeta = jnp.exp(log_g[:, None] * (RET_CHUNK - 1 - idx)[None, :])
    xi = jnp.exp(log_g[:, None] * (idx + 1.0)[None, :])
    decay = jnp.exp(log_g * RET_CHUNK)
    bc = lambda a: jnp.broadcast_to(a[:, :, None], (RET_HEADS, RET_CHUNK, LANES))
    dec_b = jnp.broadcast_to(decay[:, None, None], (RET_HEADS, RET_CHUNK, LANES))
    return cos, sin, dmask, bc(xi), bc(zeta), dec_b


def _retention(proj, gn_g, batch, seq):
    cos, sin, dmask, xi_b, ze_b, de_b = _retention_tables(seq)
    slab = lambda off: pl.BlockSpec((seq, LANES), lambda b, h: (b, off // LANES + h))
    table = pl.BlockSpec((seq, LANES), lambda b, h: (0, 0))
    per_head = pl.BlockSpec((1, RET_CHUNK, LANES), lambda b, h: (h, 0, 0))
    return pl.pallas_call(
        _retention_kernel,
        grid=(batch, RET_HEADS),
        in_specs=[slab(QK_OFF), slab(RV_OFF), slab(RG_OFF), table, table,
                  per_head, per_head, per_head, per_head,
                  pl.BlockSpec((1, LANES), lambda b, h: (0, h))],
        out_specs=pl.BlockSpec((seq, LANES), lambda b, h: (b, h)),
        out_shape=jax.ShapeDtypeStruct((batch * seq, RET_W), BF16),
        compiler_params=_cparams(("parallel", "parallel")),
        name="retention",
    )(proj, proj, proj, cos, sin, dmask, xi_b, ze_b, de_b, gn_g)


def _compress_kernel(a_ref, pe_ref, w1_ref, w2_ref, o_ref):
    n_cmp = a_ref.shape[0] // CMP_STRIDE
    top = jnp.zeros((n_cmp, CMP_HIDDEN), F32)
    bot = jnp.zeros((n_cmp, CMP_HIDDEN), F32)
    for l in range(CMP_STRIDE):
        xl = a_ref[pl.ds(l, n_cmp, stride=CMP_STRIDE), :]
        top = top + jnp.dot(xl + pe_ref[0, l:l + 1, :], w1_ref[0, l * NSA_DH:(l + 1) * NSA_DH, :],
                            preferred_element_type=F32)
        lb = CMP_STRIDE + l
        bot = bot + jnp.dot(xl + pe_ref[0, lb:lb + 1, :], w1_ref[0, lb * NSA_DH:(lb + 1) * NSA_DH, :],
                            preferred_element_type=F32)
    hidden = jax.nn.gelu(top + pltpu.roll(bot, n_cmp - 1, 0))
    o_ref[0, 0, 0] = jnp.dot(hidden, w2_ref[0], preferred_element_type=F32)


def _compress(proj, cmp_pe, cmp_w1, cmp_w2, batch, seq):
    n_cmp = seq // CMP_STRIDE
    return pl.pallas_call(
        _compress_kernel,
        grid=(batch, 2, NSA_GROUPS),
        in_specs=[
            pl.BlockSpec((seq, LANES), lambda b, s, g: (b, NKV_OFF // LANES + s * NSA_GROUPS + g)),
            pl.BlockSpec((1, CMP_LEN, NSA_DH), lambda b, s, g: (s, 0, 0)),
            pl.BlockSpec((1, CMP_LEN * NSA_DH, CMP_HIDDEN), lambda b, s, g: (s, 0, 0)),
            pl.BlockSpec((1, CMP_HIDDEN, NSA_DH), lambda b, s, g: (s, 0, 0)),
        ],
        out_specs=pl.BlockSpec((1, 1, 1, n_cmp, NSA_DH), lambda b, s, g: (b, s, g, 0, 0)),
        out_shape=jax.ShapeDtypeStruct((batch, 2, NSA_GROUPS, n_cmp, NSA_DH), F32),
        compiler_params=_cparams(("parallel", "parallel", "parallel")),
        name="compress_kv",
    )(proj, cmp_pe, cmp_w1, cmp_w2)


def _nsa_kernel(q_ref, kc_ref, vc_ref, ks_ref, vs_ref, kw_ref, vw_ref, gate_ref, cb_ref, tb_ref, ov_ref,
                og_ref, o_ref, m_sc, l_sc, acc_sc):
    qi = pl.program_id(2)
    t0 = qi * Q_TILE
    row = lax.broadcasted_iota(jnp.int32, (Q_TILE, LANES), 0)
    lane = lax.broadcasted_iota(jnp.int32, (Q_TILE, LANES), 1)
    t = t0 + row
    q_all = q_ref[...] * (NSA_DH ** -0.5)
    q_heads = [q_all[:, h * NSA_DH:(h + 1) * NSA_DH] for h in range(NSA_HPG)]

    kc = kc_ref[0, 0, 0]
    vc = vc_ref[0, 0, 0]
    cmask = t >= lane * CMP_STRIDE + (CMP_LEN - 1)
    cmask_f = cmask.astype(F32)
    o_cmp = []
    p_sum = jnp.zeros((Q_TILE, LANES), F32)
    for h in range(NSA_HPG):
        s = jnp.where(cmask, _nt_dot(q_heads[h], kc) + cb_ref[0, h], -1e30)
        m = jnp.max(s, axis=-1, keepdims=True)
        e = jnp.exp(s - m) * cmask_f
        p = e / jnp.maximum(jnp.sum(e, axis=-1, keepdims=True), 1e-30)
        o_cmp.append(jnp.dot(p, vc, preferred_element_type=F32))
        p_sum = p_sum + p

    n_sel = LANES // 4
    imp = jnp.dot(p_sum, ov_ref[...], preferred_element_type=F32, precision=lax.Precision.HIGHEST)
    blk = lane & (n_sel - 1)
    cur = t >> int(math.log2(SEL_BLOCK))
    forced = (blk == 0) | (blk == cur) | (blk == cur - 1)
    imp = jnp.where(forced, 1e9, imp)
    imp = jnp.where(blk * SEL_BLOCK <= t, imp, -1e9)
    rank = jnp.zeros((Q_TILE, LANES), jnp.int32)
    for r in range(1, n_sel):
        other = pltpu.roll(imp, r, 1)
        beats = (other > imp) | ((other == imp) & (blk >= r))
        rank = rank + beats.astype(jnp.int32)
    selected = (rank < SEL_TOPK) & (imp > -1e8) & (lane < n_sel)
    sel_bf = jnp.where(selected, 1.0, 0.0).astype(BF16)

    q_bf = [q.astype(BF16) for q in q_heads]

    def attend(branch, k_ref, v_ref, lo, hi, mask_fn):
        m_sc[branch] = jnp.full(m_sc.shape[1:], -1e30, F32)
        l_sc[branch] = jnp.zeros(l_sc.shape[1:], F32)
        acc_sc[branch] = jnp.zeros(acc_sc.shape[1:], F32)

        def body(jt, carry):
            k0 = pl.multiple_of(jt * Q_TILE, Q_TILE)
            kt = k_ref[pl.ds(k0, Q_TILE), :].astype(BF16)
            vt = v_ref[pl.ds(k0, Q_TILE), :].astype(BF16)
            valid = mask_fn(jt, k0 + lane)
            valid_f = valid.astype(F32)
            d = jnp.clip(qi - jt, 0, 2)
            for h in range(NSA_HPG):
                s = jnp.where(valid, _nt_dot(q_bf[h], kt) + tb_ref[0, h, d], -1e30)
                m_prev = m_sc[branch, h]
                m_new = jnp.maximum(m_prev, jnp.max(s, axis=-1, keepdims=True))
                alpha = jnp.exp(m_prev - m_new)
                p = jnp.exp(s - m_new) * valid_f
                l_sc[branch, h] = alpha * l_sc[branch, h] + jnp.sum(p, axis=-1, keepdims=True)
                acc_sc[branch, h] = alpha * acc_sc[branch, h] + jnp.dot(p.astype(BF16), vt,
                                                                      preferred_element_type=F32)
                m_sc[branch, h] = m_new
            return carry

        lax.fori_loop(lo, hi, body, 0)

    def sel_mask(jt, kpos):
        expand = (row == 2 * jt + (lane >> int(math.log2(SEL_BLOCK)))).astype(BF16)
        hit = jnp.dot(sel_bf, expand, preferred_element_type=F32)
        return (hit > 0.5) & (t >= kpos)

    def win_mask(jt, kpos):
        return (t >= kpos) & (t - kpos < WINDOW)

    attend(0, ks_ref, vs_ref, 0, qi + 1, sel_mask)
    attend(1, kw_ref, vw_ref, jnp.maximum(qi - WINDOW // Q_TILE, 0), qi + 1, win_mask)

    gates = jax.nn.sigmoid(gate_ref[...])
    og = og_ref[...]
    for h in range(NSA_HPG):
        c = 3 * h
        o_sel = acc_sc[0, h] / jnp.maximum(l_sc[0, h], 1e-30)
        o_win = acc_sc[1, h] / jnp.maximum(l_sc[1, h], 1e-30)
        o = gates[:, c:c + 1] * o_cmp[h] + gates[:, c + 1:c + 2] * o_sel + gates[:, c + 2:c + 3] * o_win
        ms = jnp.mean(o * o, axis=-1, keepdims=True)
        o_ref[:, h * NSA_DH:(h + 1) * NSA_DH] = (
            o * lax.rsqrt(ms + EPS) * og[:, h * NSA_DH:(h + 1) * NSA_DH]).astype(o_ref.dtype)


def _nsa_tables(rel_bias, seq):
    rb = rel_bias.reshape(N_BUCKETS, NSA_GROUPS, NSA_HPG)
    n_cmp = seq // CMP_STRIDE
    end = jnp.arange(n_cmp) * CMP_STRIDE + CMP_LEN - 1
    dist_c = jnp.arange(seq)[:, None] - end[None, :]
    cmp_bias = jnp.moveaxis(rb[_t5_bucket(dist_c)], (2, 3), (0, 1))
    i = jnp.arange(Q_TILE)
    dist_t = jnp.arange(3)[:, None, None] * Q_TILE + i[None, :, None] - i[None, None, :]
    tile_bias = jnp.moveaxis(rb[_t5_bucket(dist_t)], (3, 4), (0, 1))
    n_sel = seq // SEL_BLOCK
    cs = np.arange(n_cmp) * CMP_STRIDE
    js = np.arange(n_sel) * SEL_BLOCK
    overlap = np.clip(np.minimum(cs[:, None] + CMP_LEN, js[None, :] + SEL_BLOCK)
                      - np.maximum(cs[:, None], js[None, :]), 0, None).astype(np.float32) / CMP_LEN
    overlap[n_cmp - 1, :] = 0.0
    overlap = np.tile(overlap, (1, LANES // n_sel))
    return cmp_bias, tile_bias, jnp.asarray(overlap)


def _nsa(proj, cmp_kv, rel_bias, out_g, batch, seq):
    assert seq // SEL_BLOCK == LANES // 4 and seq // CMP_STRIDE == LANES
    cmp_bias, tile_bias, overlap = _nsa_tables(rel_bias, seq)
    nq = seq // Q_TILE
    kv_slab = lambda s: pl.BlockSpec((seq, LANES), lambda b, g, i: (b, NKV_OFF // LANES + s * NSA_GROUPS + g))
    cmp_spec = lambda s: pl.BlockSpec((1, 1, 1, LANES, NSA_DH), lambda b, g, i: (b, s, g, 0, 0))
    group_w = NSA_HPG * NSA_DH
    return pl.pallas_call(
        _nsa_kernel,
        grid=(batch, NSA_GROUPS, nq),
        in_specs=[
            pl.BlockSpec((Q_TILE, group_w), lambda b, g, i: (b * nq + i, NQ_OFF // group_w + g)),
            cmp_spec(0), cmp_spec(1),
            kv_slab(2), kv_slab(3), kv_slab(4), kv_slab(5),
            pl.BlockSpec((Q_TILE, LANES), lambda b, g, i: (b * nq + i, NGATE_OFF // LANES + g)),
            pl.BlockSpec((1, NSA_HPG, Q_TILE, LANES), lambda b, g, i: (g, 0, i, 0)),
            pl.BlockSpec((1, NSA_HPG, 3, Q_TILE, Q_TILE), lambda b, g, i: (g, 0, 0, 0, 0)),
            pl.BlockSpec((LANES, LANES), lambda b, g, i: (0, 0)),
            pl.BlockSpec((1, group_w), lambda b, g, i: (0, g)),
        ],
        out_specs=pl.BlockSpec((Q_TILE, group_w), lambda b, g, i: (b * nq + i, g)),
        out_shape=jax.ShapeDtypeStruct((batch * seq, NSA_W), BF16),
        scratch_shapes=[pltpu.VMEM((2, NSA_HPG, Q_TILE, LANES), F32),
                        pltpu.VMEM((2, NSA_HPG, Q_TILE, LANES), F32),
                        pltpu.VMEM((2, NSA_HPG, Q_TILE, NSA_DH), F32)],
        compiler_params=_cparams(("parallel", "parallel", "arbitrary")),
        name="sparse_attention",
    )(proj, cmp_kv, cmp_kv, proj, proj, proj, proj, proj, cmp_bias, tile_bias, overlap, out_g)


def _outproj_kernel(x_ref, ret_ref, nsa_ref, wr_ref, wn_ref, g_ref, h_ref, xn_ref):
    h = (x_ref[...] + jnp.dot(ret_ref[...], wr_ref[...], preferred_element_type=F32)
         + jnp.dot(nsa_ref[...], wn_ref[...], preferred_element_type=F32))
    h_ref[...] = h
    ms = jnp.mean(h * h, axis=-1, keepdims=True)
    xn_ref[...] = h * lax.rsqrt(ms + EPS) * g_ref[...]


def _out_projection(x2, ret, nsa, w_ret, w_nsa, g2, tm=256):
    n = x2.shape[0]
    tm = min(tm, n)
    row = lambda w: pl.BlockSpec((tm, w), lambda i: (i, 0))
    whole = lambda a: pl.BlockSpec(a.shape, lambda i: (0, 0))
    return pl.pallas_call(
        _outproj_kernel,
        grid=(n // tm,),
        in_specs=[row(D_MODEL), row(RET_W), row(NSA_W), whole(w_ret), whole(w_nsa), whole(g2)],
        out_specs=[row(D_MODEL), row(D_MODEL)],
        out_shape=[jax.ShapeDtypeStruct((n, D_MODEL), F32), jax.ShapeDtypeStruct((n, D_MODEL), F32)],
        compiler_params=_cparams(("parallel",)),
        name="out_projection",
    )(x2, ret, nsa, w_ret, w_nsa, g2)


def _top_k_rows(vals, k, payload=None):
    rows = lax.broadcasted_iota(jnp.int32, vals.shape, 0)
    big = vals.shape[0]
    out_row = lax.broadcasted_iota(jnp.int32, (k, vals.shape[1]), 0)
    out_v = jnp.zeros((k, vals.shape[1]), F32)
    out_i = jnp.zeros((k, vals.shape[1]), jnp.int32)
    for r in range(k):
        m = jnp.max(vals, axis=0, keepdims=True)
        pos = jnp.min(jnp.where(vals == m, rows, big), axis=0, keepdims=True)
        hit = rows == pos
        picked = pos if payload is None else jnp.sum(jnp.where(hit, payload, 0), axis=0, keepdims=True)
        out_v = jnp.where(out_row == r, m, out_v)
        out_i = jnp.where(out_row == r, picked, out_i)
        vals = jnp.where(hit, -jnp.inf, vals)
    return out_v, out_i


def _route_kernel(xn_ref, wq_ref, sk_ref, idx_ref, gate_ref):
    pq = jnp.dot(xn_ref[...].astype(BF16), wq_ref[...], preferred_element_type=F32)
    half = PEER_DQ // 2
    for h in range(PEER_HEADS):
        sv, si = [], []
        for p in range(2):
            c0 = (2 * h + p) * half
            scores = _nt_dot(sk_ref[p], pq[:, c0:c0 + half])
            v, i = _top_k_rows(scores, PEER_TOPK)
            sv.append(v)
            si.append(i)
        cand = jnp.concatenate([sv[0][a:a + 1, :] + sv[1] for a in range(PEER_TOPK)], axis=0)
        cidx = jnp.concatenate([si[0][a:a + 1, :] * PEER_NKEYS + si[1] for a in range(PEER_TOPK)], axis=0)
        cs, eidx = _top_k_rows(cand, PEER_TOPK, payload=cidx)
        e = jnp.exp(cs - jnp.max(cs, axis=0, keepdims=True))
        gate_ref[0, h] = e / jnp.sum(e, axis=0, keepdims=True)
        idx_ref[0, h] = eidx


def _peer_route(xn2, wq_bf, subkeys, tm=256):
    n = xn2.shape[0]
    tm = min(tm, n)
    out_block = pl.BlockSpec((1, PEER_HEADS, PEER_TOPK, tm), lambda i: (i, 0, 0, 0))
    shape = (n // tm, PEER_HEADS, PEER_TOPK, tm)
    idx, gate = pl.pallas_call(
        _route_kernel,
        grid=(n // tm,),
        in_specs=[pl.BlockSpec((tm, D_MODEL), lambda i: (i, 0)),
                  pl.BlockSpec(wq_bf.shape, lambda i: (0, 0)),
                  pl.BlockSpec(subkeys.shape, lambda i: (0, 0, 0))],
        out_specs=[out_block, out_block],
        out_shape=[jax.ShapeDtypeStruct(shape, jnp.int32), jax.ShapeDtypeStruct(shape, F32)],
        compiler_params=_cparams(("parallel",)),
        name="peer_route",
    )(xn2, wq_bf, subkeys)
    to_tokens = lambda a: a.transpose(0, 3, 1, 2).reshape(n, PEER_EXPERTS_PER_TOKEN)
    return to_tokens(idx), to_tokens(gate)


PEER_TOKENS_PER_STEP = 64
PEER_TOKENS_PER_SLOT = 8
PEER_ISSUE_UNROLL = 16


def _peer_kernel(idx_ref, gate_ref, xn_ref, h_ref, gf_ref, uv_hbm, o_ref, buf, sem, mix, mix8):
    rows_per_slot = PEER_TOKENS_PER_SLOT * PEER_EXPERTS_PER_TOKEN
    n_sub = PEER_TOKENS_PER_STEP // PEER_TOKENS_PER_SLOT
    eye = (lax.broadcasted_iota(jnp.int32, (LANES, LANES), 0)
           == lax.broadcasted_iota(jnp.int32, (LANES, LANES), 1)).astype(F32)

    def issue(sub, slot):
        base = sub * rows_per_slot

        def body(i, carry):
            for j in range(PEER_ISSUE_UNROLL):
                k = i * PEER_ISSUE_UNROLL + j
                e = idx_ref[base + k]
                pltpu.make_async_copy(uv_hbm.at[pl.ds(e, 1), :], buf.at[slot, pl.ds(k, 1), :], sem.at[slot]).start()
            return carry

        lax.fori_loop(0, rows_per_slot // PEER_ISSUE_UNROLL, body, 0)

    def wait(slot):
        pltpu.make_async_copy(uv_hbm.at[pl.ds(0, rows_per_slot), :], buf.at[slot], sem.at[slot]).wait()

    def compute(sub, slot):
        rows = pl.ds(pl.multiple_of(sub * PEER_TOKENS_PER_SLOT, PEER_TOKENS_PER_SLOT), PEER_TOKENS_PER_SLOT)
        x8 = xn_ref[rows, :]
        g8 = gate_ref[rows, :]
        for tok in range(PEER_TOKENS_PER_SLOT):
            e0 = tok * PEER_EXPERTS_PER_TOKEN
            acc = jnp.zeros((PEER_EXPERTS_PER_TOKEN, LANES), F32)
            for c in range(D_MODEL // LANES):
                cols = slice(c * LANES, (c + 1) * LANES)
                acc = acc + buf[slot, e0:e0 + PEER_EXPERTS_PER_TOKEN, cols] * x8[tok:tok + 1, cols]
            hid = jnp.sum(acc, axis=-1, keepdims=True)
            g_col = jnp.sum(eye * g8[tok:tok + 1, :], axis=-1, keepdims=True)
            w = jnp.broadcast_to(jax.nn.gelu(hid) * g_col, (PEER_EXPERTS_PER_TOKEN, LANES))
            for c in range(D_MODEL // LANES):
                vc = buf[slot, e0:e0 + PEER_EXPERTS_PER_TOKEN, D_MODEL + c * LANES:D_MODEL + (c + 1) * LANES]
                mix8[tok:tok + 1, c * LANES:(c + 1) * LANES] = jnp.sum(vc * w, axis=0, keepdims=True)
        mix[rows, :] = mix8[...]

    issue(0, 0)

    def sub_step(sub, carry):
        slot = sub % 2

        @pl.when(sub + 1 < n_sub)
        def _():
            issue(sub + 1, 1 - slot)

        wait(slot)
        compute(sub, slot)
        return carry

    lax.fori_loop(0, n_sub, sub_step, 0)
    h2 = h_ref[...] + mix[...]
    ms = jnp.mean(h2 * h2, axis=-1, keepdims=True)
    o_ref[...] = h2 * lax.rsqrt(ms + EPS) * gf_ref[...]


def _peer_experts(idx, gate, xn2, h1, gf, uv):
    n = xn2.shape[0]
    tb = PEER_TOKENS_PER_STEP
    assert n % tb == 0
    rows_per_slot = PEER_TOKENS_PER_SLOT * PEER_EXPERTS_PER_TOKEN
    row = lambda w: pl.BlockSpec((tb, w), lambda i: (i, 0))
    return pl.pallas_call(
        _peer_kernel,
        grid=(n // tb,),
        in_specs=[
            pl.BlockSpec((tb * PEER_EXPERTS_PER_TOKEN,), lambda i: (i,), memory_space=pltpu.SMEM),
            row(PEER_EXPERTS_PER_TOKEN), row(D_MODEL), row(D_MODEL),
            pl.BlockSpec((1, D_MODEL), lambda i: (0, 0)),
            pl.BlockSpec(memory_space=pl.ANY),
        ],
        out_specs=row(D_MODEL),
        out_shape=jax.ShapeDtypeStruct((n, D_MODEL), F32),
        scratch_shapes=[pltpu.VMEM((2, rows_per_slot, 2 * D_MODEL), F32),
                        pltpu.SemaphoreType.DMA((2,)),
                        pltpu.VMEM((tb, D_MODEL), F32),
                        pltpu.VMEM((PEER_TOKENS_PER_SLOT, D_MODEL), F32)],
        compiler_params=_cparams(("arbitrary",)),
        name="peer_experts",
    )(idx.reshape(-1), gate, xn2, h1, gf, uv)


def kernel(x, norm1_g, w_in, rel_bias, ret_gn_g, cmp_pe, cmp_w1, cmp_w2, nsa_out_g, w_out, norm2_g, peer_wq,
           peer_subkeys, peer_u, peer_v, norm_f_g):
    batch, seq, _ = x.shape
    n = batch * seq
    depth = w_in.shape[0]
    assert depth == 1, "the final RMSNorm is fused into the (single) layer's PEER stage"
    src = _proj_column_source()
    gather_cols = jnp.asarray(np.maximum(src, 0), jnp.int32)
    col_used = jnp.asarray(src >= 0)
    h = x.reshape(n, D_MODEL)
    for l in range(depth):
        w_perm = jnp.where(col_used[None, :], w_in[l][:, gather_cols], 0.0).astype(BF16)
        proj = _in_projection(h, norm1_g[l][None, :], w_perm)
        ret = _retention(proj, ret_gn_g[l][None, :], batch, seq)
        cmp_kv = _compress(proj, cmp_pe[l], cmp_w1[l], cmp_w2[l], batch, seq)
        nsa = _nsa(proj, cmp_kv, rel_bias, nsa_out_g[l][None, :], batch, seq)
        w_o = w_out[l].astype(BF16)
        h1, xn2 = _out_projection(h, ret, nsa, w_o[:RET_W], w_o[RET_W:], norm2_g[l][None, :])
        idx, gate = _peer_route(xn2, peer_wq[l].astype(BF16), peer_subkeys[l])
        uv = jnp.concatenate([peer_u[l], peer_v[l]], axis=1)
        h = _peer_experts(idx, gate, xn2, h1, norm_f_g[None, :], uv)
    return h.reshape(batch, seq, D_MODEL)
```

```python
import math

import numpy as np
import jax
import jax.numpy as jnp
from jax import lax
from jax.experimental import pallas as pl
from jax.experimental.pallas import tpu as pltpu

F32 = jnp.float32
BF16 = jnp.bfloat16

LANES = 128
SUBLANES = 8
VMEM_LIMIT_BYTES = 56 * 1024 * 1024

D_MODEL = 2048
EPS = 1e-6
D_MIX = D_MODEL
RET_HEADS = 8
RET_DV = D_MIX // (2 * RET_HEADS)
RET_DK = RET_DV // 2
RET_CHUNK = 128
RET_W = RET_HEADS * RET_DV
NSA_HEADS = 8
NSA_GROUPS = 2
NSA_HPG = NSA_HEADS // NSA_GROUPS
NSA_DH = D_MIX // (2 * NSA_HEADS)
NSA_W = NSA_HEADS * NSA_DH
KV_W = NSA_GROUPS * NSA_DH
CMP_LEN = 32
CMP_STRIDE = 16
CMP_HIDDEN = 2 * NSA_DH
SEL_BLOCK = 64
SEL_TOPK = 16
WINDOW = 512
N_BUCKETS = 32
MAX_DISTANCE = 128
PEER_HEADS = 8
PEER_NKEYS = 128
PEER_DQ = 256
PEER_TOPK = 16
PEER_EXPERTS_PER_TOKEN = PEER_HEADS * PEER_TOPK

QK_OFF = 0
RV_OFF = QK_OFF + RET_HEADS * LANES
RG_OFF = RV_OFF + RET_W
NQ_OFF = RG_OFF + RET_W
NKV_OFF = NQ_OFF + NSA_W
NGATE_OFF = NKV_OFF + 6 * KV_W
PROJ_W = 6144
Q_TILE = 128
QH_ROWS = NSA_HPG * Q_TILE


def _permuted_in_weight(w):
    rq0, rk0 = 0, RET_HEADS * RET_DK
    rv0 = 2 * RET_HEADS * RET_DK
    nkv_end = rv0 + 2 * RET_W + NSA_W + 6 * KV_W
    per_group = NSA_HPG * 3
    pieces = []
    for h in range(RET_HEADS):
        pieces.append(w[:, rq0 + h * RET_DK:rq0 + (h + 1) * RET_DK])
        pieces.append(w[:, rk0 + h * RET_DK:rk0 + (h + 1) * RET_DK])
    pieces.append(w[:, rv0:nkv_end])
    for g in range(NSA_GROUPS):
        pieces.append(w[:, nkv_end + g * per_group:nkv_end + (g + 1) * per_group])
        pieces.append(jnp.zeros((w.shape[0], LANES - per_group), w.dtype))
    used = NGATE_OFF + NSA_GROUPS * LANES
    pieces.append(jnp.zeros((w.shape[0], PROJ_W - used), w.dtype))
    return jnp.concatenate(pieces, axis=1)


def _cparams(sem, vmem=VMEM_LIMIT_BYTES):
    return pltpu.CompilerParams(dimension_semantics=sem, vmem_limit_bytes=vmem)


def _nt_dot(a, b, **kw):
    return lax.dot_general(a, b, (((1,), (1,)), ((), ())), preferred_element_type=F32, **kw)


def _tn_dot(a, b):
    return lax.dot_general(a, b, (((0,), (0,)), ((), ())), preferred_element_type=F32)


def _t5_bucket(dist):
    dist = jnp.maximum(dist, 0)
    max_exact = N_BUCKETS // 2
    log_ratio = jnp.log(jnp.maximum(dist, 1).astype(F32) / max_exact) / math.log(MAX_DISTANCE / max_exact)
    large = jnp.minimum(max_exact + (log_ratio * (N_BUCKETS - max_exact)).astype(jnp.int32), N_BUCKETS - 1)
    return jnp.where(dist < max_exact, dist, large)


def _inproj_kernel(x_ref, g_ref, w_ref, o_ref, xn_ref):
    @pl.when(pl.program_id(1) == 0)
    def _():
        x = x_ref[...]
        ms = jnp.mean(x * x, axis=-1, keepdims=True)
        xn_ref[...] = (x * lax.rsqrt(ms + EPS) * g_ref[...]).astype(BF16)

    o_ref[...] = jnp.dot(xn_ref[...], w_ref[...], preferred_element_type=F32)


def _in_projection(x2, g1, w_perm, tm=1024, tn=1024):
    n = x2.shape[0]
    tm = min(tm, n)
    return pl.pallas_call(
        _inproj_kernel,
        grid=(n // tm, PROJ_W // tn),
        in_specs=[
            pl.BlockSpec((tm, D_MODEL), lambda i, j: (i, 0)),
            pl.BlockSpec((1, D_MODEL), lambda i, j: (0, 0)),
            pl.BlockSpec((D_MODEL, tn), lambda i, j: (0, j)),
        ],
        out_specs=pl.BlockSpec((tm, tn), lambda i, j: (i, j)),
        out_shape=jax.ShapeDtypeStruct((n, PROJ_W), F32),
        scratch_shapes=[pltpu.VMEM((tm, D_MODEL), BF16)],
        compiler_params=_cparams(("parallel", "arbitrary")),
        name="in_projection",
    )(x2, g1, w_perm)


def _retention_kernel(qk_ref, v_ref, g_ref, cos_ref, sin_ref, dm_ref, xi_ref, ze_ref, de_ref, gn_ref, o_ref):
    seq = qk_ref.shape[0]
    lane = lax.broadcasted_iota(jnp.int32, (RET_CHUNK, LANES), 1)
    first_half = (lane & (RET_DK - 1)) < (RET_DK // 2)
    q_scale = jnp.where(lane < RET_DK, RET_DK ** -0.5, 0.0).astype(F32)
    dm = dm_ref[0]
    xi = xi_ref[0]
    ze = ze_ref[0]
    de = de_ref[0]
    gn = gn_ref[...]

    def body(c, state):
        r0 = pl.multiple_of(c * RET_CHUNK, RET_CHUNK)
        rows = pl.ds(r0, RET_CHUNK)
        z = qk_ref[rows, :]
        partner = jnp.where(first_half, pltpu.roll(z, LANES - RET_DK // 2, 1), pltpu.roll(z, RET_DK // 2, 1))
        rot = z * cos_ref[rows, :] + partner * sin_ref[rows, :]
        qm = rot * q_scale
        kk = pltpu.roll(rot, RET_DK, 1)
        v = v_ref[rows, :]
        s = _nt_dot(qm, kk) * dm
        y = jnp.dot(s, v, preferred_element_type=F32) + jnp.dot(qm * xi, state, preferred_element_type=F32)
        kv = _tn_dot(kk * ze, v)
        mu = jnp.mean(y, axis=-1, keepdims=True)
        yc = y - mu
        var = jnp.mean(yc * yc, axis=-1, keepdims=True)
        yn = yc * lax.rsqrt(var + EPS) * gn
        gate = g_ref[rows, :]
        o_ref[rows, :] = (gate * jax.nn.sigmoid(gate) * yn).astype(o_ref.dtype)
        return state * de + kv

    lax.fori_loop(0, seq // RET_CHUNK, body, jnp.zeros((LANES, RET_DV), F32))


def _retention_tables(seq):
    half = RET_DK // 2
    pos = jnp.arange(seq, dtype=F32)
    inv = 1.0 / (10000.0 ** (jnp.arange(half, dtype=F32) / half))
    ang = pos[:, None] * inv[None, :]
    lane = np.arange(LANES)
    cos = jnp.cos(ang)[:, lane % half]
    sign = np.where((lane % (2 * half)) < half, -1.0, 1.0).astype(np.float32)
    sin = jnp.sin(ang)[:, lane % half] * sign[None, :]
    log_g = jnp.log(1.0 - 2.0 ** (-5.0 - jnp.arange(RET_HEADS, dtype=F32)))
    idx = jnp.arange(RET_CHUNK, dtype=F32)
    diff = idx[:, None] - idx[None, :]
    dmask = jnp.where(diff >= 0, jnp.exp(log_g[:, None, None] * jnp.maximum(diff, 0.0)), 0.0)
    zeta = jnp.exp(log_g[:, None] * (RET_CHUNK - 1 - idx)[None, :])
    xi = jnp.exp(log_g[:, None] * (idx + 1.0)[None, :])
    decay = jnp.exp(log_g * RET_CHUNK)
    bc = lambda a: jnp.broadcast_to(a[:, :, None], (RET_HEADS, RET_CHUNK, LANES))
    dec_b = jnp.broadcast_to(decay[:, None, None], (RET_HEADS, RET_CHUNK, LANES))
    return cos, sin, dmask, bc(xi), bc(zeta), dec_b


def _retention(proj, gn_g, batch, seq):
    cos, sin, dmask, xi_b, ze_b, de_b = _retention_tables(seq)
    slab = lambda off: pl.BlockSpec((seq, LANES), lambda b, h: (b, off // LANES + h))
    table = pl.BlockSpec((seq, LANES), lambda b, h: (0, 0))
    per_head = pl.BlockSpec((1, RET_CHUNK, LANES), lambda b, h: (h, 0, 0))
    return pl.pallas_call(
        _retention_kernel,
        grid=(batch, RET_HEADS),
        in_specs=[slab(QK_OFF), slab(RV_OFF), slab(RG_OFF), table, table,
                  per_head, per_head, per_head, per_head,
                  pl.BlockSpec((1, LANES), lambda b, h: (0, h))],
        out_specs=pl.BlockSpec((seq, LANES), lambda b, h: (b, h)),
        out_shape=jax.ShapeDtypeStruct((batch * seq, RET_W), BF16),
        compiler_params=_cparams(("parallel", "parallel")),
        name="retention",
    )(proj, proj, proj, cos, sin, dmask, xi_b, ze_b, de_b, gn_g)


def _compress_kernel(a_ref, pe_ref, w1_ref, w2_ref, o_ref):
    n_cmp = a_ref.shape[0] // CMP_STRIDE
    top = jnp.zeros((n_cmp, CMP_HIDDEN), F32)
    bot = jnp.zeros((n_cmp, CMP_HIDDEN), F32)
    for l in range(CMP_STRIDE):
        xl = a_ref[pl.ds(l, n_cmp, stride=CMP_STRIDE), :]
        top = top + jnp.dot(xl + pe_ref[0, l:l + 1, :], w1_ref[0, l * NSA_DH:(l + 1) * NSA_DH, :],
                            preferred_element_type=F32)
        lb = CMP_STRIDE + l
        bot = bot + jnp.dot(xl + pe_ref[0, lb:lb + 1, :], w1_ref[0, lb * NSA_DH:(lb + 1) * NSA_DH, :],
                            preferred_element_type=F32)
    hidden = jax.nn.gelu(top + pltpu.roll(bot, n_cmp - 1, 0))
    o_ref[0, 0, 0] = jnp.dot(hidden, w2_ref[0], preferred_element_type=F32)


def _compress(proj, cmp_pe, cmp_w1, cmp_w2, batch, seq):
    n_cmp = seq // CMP_STRIDE
    return pl.pallas_call(
        _compress_kernel,
        grid=(batch, 2, NSA_GROUPS),
        in_specs=[
            pl.BlockSpec((seq, LANES), lambda b, s, g: (b, NKV_OFF // LANES + s * NSA_GROUPS + g)),
            pl.BlockSpec((1, CMP_LEN, NSA_DH), lambda b, s, g: (s, 0, 0)),
            pl.BlockSpec((1, CMP_LEN * NSA_DH, CMP_HIDDEN), lambda b, s, g: (s, 0, 0)),
            pl.BlockSpec((1, CMP_HIDDEN, NSA_DH), lambda b, s, g: (s, 0, 0)),
        ],
        out_specs=pl.BlockSpec((1, 1, 1, n_cmp, NSA_DH), lambda b, s, g: (b, s, g, 0, 0)),
        out_shape=jax.ShapeDtypeStruct((batch, 2, NSA_GROUPS, n_cmp, NSA_DH), F32),
        compiler_params=_cparams(("parallel", "parallel", "parallel")),
        name="compress_kv",
    )(proj, cmp_pe, cmp_w1, cmp_w2)


def _stack_heads(a):
    return jnp.concatenate([a[:, h * NSA_DH:(h + 1) * NSA_DH] for h in range(NSA_HPG)], axis=0)


def _nsa_kernel(q_ref, kc_ref, vc_ref, ks_ref, vs_ref, kw_ref, vw_ref, gate_ref, cb_ref, tb_ref, ov_ref,
                og_ref, o_ref, m_sc, l_sc, acc_sc):
    qi = pl.program_id(2)
    row = lax.broadcasted_iota(jnp.int32, (Q_TILE, LANES), 0)
    lane = lax.broadcasted_iota(jnp.int32, (Q_TILE, LANES), 1)
    t = qi * Q_TILE + row
    lane_s = lax.broadcasted_iota(jnp.int32, (QH_ROWS, LANES), 1)
    t_s = qi * Q_TILE + (lax.broadcasted_iota(jnp.int32, (QH_ROWS, LANES), 0) & (Q_TILE - 1))
    q = _stack_heads(q_ref[...] * (NSA_DH ** -0.5))

    cmask = t_s >= lane_s * CMP_STRIDE + (CMP_LEN - 1)
    s = jnp.where(cmask, _nt_dot(q, kc_ref[0, 0, 0]) + cb_ref[0, 0], -1e30)
    e = jnp.exp(s - jnp.max(s, axis=-1, keepdims=True)) * cmask.astype(F32)
    p = e / jnp.maximum(jnp.sum(e, axis=-1, keepdims=True), 1e-30)
    o_cmp = jnp.dot(p, vc_ref[0, 0, 0], preferred_element_type=F32)
    p_sum = p[0:Q_TILE]
    for h in range(1, NSA_HPG):
        p_sum = p_sum + p[h * Q_TILE:(h + 1) * Q_TILE]

    n_sel = LANES // 4
    imp = jnp.dot(p_sum, ov_ref[...], preferred_element_type=F32, precision=lax.Precision.HIGHEST)
    blk = lane & (n_sel - 1)
    cur = t >> int(math.log2(SEL_BLOCK))
    forced = (blk == 0) | (blk == cur) | (blk == cur - 1)
    imp = jnp.where(forced, 1e9, imp)
    imp = jnp.where(blk * SEL_BLOCK <= t, imp, -1e9)
    rank = jnp.zeros((Q_TILE, LANES), jnp.int32)
    for r in range(1, n_sel):
        other = pltpu.roll(imp, r, 1)
        beats = (other > imp) | ((other == imp) & (blk >= r))
        rank = rank + beats.astype(jnp.int32)
    selected = (rank < SEL_TOPK) & (imp > -1e8) & (lane < n_sel)
    sel_one = jnp.where(selected, 1.0, 0.0).astype(BF16)
    sel_bf = jnp.concatenate([sel_one] * NSA_HPG, axis=0)

    q_bf = q.astype(BF16)

    def attend(branch, k_ref, v_ref, lo, hi, mask_fn):
        m_sc[branch] = jnp.full(m_sc.shape[1:], -1e30, F32)
        l_sc[branch] = jnp.zeros(l_sc.shape[1:], F32)
        acc_sc[branch] = jnp.zeros(acc_sc.shape[1:], F32)

        def body(jt, carry):
            k0 = pl.multiple_of(jt * Q_TILE, Q_TILE)
            kt = k_ref[pl.ds(k0, Q_TILE), :].astype(BF16)
            vt = v_ref[pl.ds(k0, Q_TILE), :].astype(BF16)
            valid = mask_fn(jt, k0 + lane_s)
            sc = jnp.where(valid, _nt_dot(q_bf, kt) + tb_ref[0, jnp.clip(qi - jt, 0, 2)], -1e30)
            m_prev = m_sc[branch]
            m_new = jnp.maximum(m_prev, jnp.max(sc, axis=-1, keepdims=True))
            alpha = jnp.exp(m_prev - m_new)
            pr = jnp.exp(sc - m_new) * valid.astype(F32)
            l_sc[branch] = alpha * l_sc[branch] + jnp.sum(pr, axis=-1, keepdims=True)
            acc_sc[branch] = alpha * acc_sc[branch] + jnp.dot(pr.astype(BF16), vt, preferred_element_type=F32)
            m_sc[branch] = m_new
            return carry

        lax.fori_loop(lo, hi, body, 0)
        return acc_sc[branch] / jnp.maximum(l_sc[branch], 1e-30)

    def sel_mask(jt, kpos):
        expand = (row == 2 * jt + (lane >> int(math.log2(SEL_BLOCK)))).astype(BF16)
        hit = jnp.dot(sel_bf, expand, preferred_element_type=F32)
        return (hit > 0.5) & (t_s >= kpos)

    def win_mask(jt, kpos):
        return (t_s >= kpos) & (t_s - kpos < WINDOW)

    o_sel = attend(0, ks_ref, vs_ref, 0, qi + 1, sel_mask)
    o_win = attend(1, kw_ref, vw_ref, jnp.maximum(qi - WINDOW // Q_TILE, 0), qi + 1, win_mask)

    gates = jax.nn.sigmoid(gate_ref[...])
    gate_col = lambda j: jnp.concatenate([gates[:, 3 * h + j:3 * h + j + 1] for h in range(NSA_HPG)], axis=0)
    o = gate_col(0) * o_cmp + gate_col(1) * o_sel + gate_col(2) * o_win
    o = o * lax.rsqrt(jnp.mean(o * o, axis=-1, keepdims=True) + EPS)
    for h in range(NSA_HPG):
        cols = slice(h * NSA_DH, (h + 1) * NSA_DH)
        o_ref[:, cols] = (o[h * Q_TILE:(h + 1) * Q_TILE] * og_ref[:, cols]).astype(o_ref.dtype)


def _bucket_lookup(rb, bucket):
    out = jnp.zeros((NSA_GROUPS, NSA_HPG) + bucket.shape, F32)
    extra = (None,) * bucket.ndim
    for b in range(N_BUCKETS):
        out = jnp.where((bucket == b)[None, None], rb[b][(...,) + extra], out)
    return out


def _nsa_tables(rel_bias, seq):
    rb = rel_bias.reshape(N_BUCKETS, NSA_GROUPS, NSA_HPG)
    n_cmp = seq // CMP_STRIDE
    nq = seq // Q_TILE
    end = jnp.arange(n_cmp) * CMP_STRIDE + CMP_LEN - 1
    dist_c = jnp.arange(seq)[:, None] - end[None, :]
    cmp_bias = _bucket_lookup(rb, _t5_bucket(dist_c))
    cmp_bias = cmp_bias.reshape(NSA_GROUPS, NSA_HPG, nq, Q_TILE, n_cmp).transpose(0, 2, 1, 3, 4)
    cmp_bias = cmp_bias.reshape(NSA_GROUPS, nq, QH_ROWS, n_cmp)
    i = jnp.arange(Q_TILE)
    dist_t = jnp.arange(3)[:, None, None] * Q_TILE + i[None, :, None] - i[None, None, :]
    tile_bias = _bucket_lookup(rb, _t5_bucket(dist_t))
    tile_bias = tile_bias.transpose(0, 2, 1, 3, 4).reshape(NSA_GROUPS, 3, QH_ROWS, Q_TILE)
    n_sel = seq // SEL_BLOCK
    cs = np.arange(n_cmp) * CMP_STRIDE
    js = np.arange(n_sel) * SEL_BLOCK
    overlap = np.clip(np.minimum(cs[:, None] + CMP_LEN, js[None, :] + SEL_BLOCK)
                      - np.maximum(cs[:, None], js[None, :]), 0, None).astype(np.float32) / CMP_LEN
    overlap[n_cmp - 1, :] = 0.0
    overlap = np.tile(overlap, (1, LANES // n_sel))
    return cmp_bias, tile_bias, jnp.asarray(overlap)


def _nsa(proj, cmp_kv, rel_bias, out_g, batch, seq):
    assert seq // SEL_BLOCK == LANES // 4 and seq // CMP_STRIDE == LANES
    cmp_bias, tile_bias, overlap = _nsa_tables(rel_bias, seq)
    nq = seq // Q_TILE
    kv_slab = lambda s: pl.BlockSpec((seq, LANES), lambda b, g, i: (b, NKV_OFF // LANES + s * NSA_GROUPS + g))
    cmp_spec = lambda s: pl.BlockSpec((1, 1, 1, LANES, NSA_DH), lambda b, g, i: (b, s, g, 0, 0))
    group_w = NSA_HPG * NSA_DH
    return pl.pallas_call(
        _nsa_kernel,
        grid=(batch, NSA_GROUPS, nq),
        in_specs=[
            pl.BlockSpec((Q_TILE, group_w), lambda b, g, i: (b * nq + i, NQ_OFF // group_w + g)),
            cmp_spec(0), cmp_spec(1),
            kv_slab(2), kv_slab(3), kv_slab(4), kv_slab(5),
            pl.BlockSpec((Q_TILE, LANES), lambda b, g, i: (b * nq + i, NGATE_OFF // LANES + g)),
            pl.BlockSpec((1, 1, QH_ROWS, LANES), lambda b, g, i: (g, i, 0, 0)),
            pl.BlockSpec((1, 3, QH_ROWS, Q_TILE), lambda b, g, i: (g, 0, 0, 0)),
            pl.BlockSpec((LANES, LANES), lambda b, g, i: (0, 0)),
            pl.BlockSpec((1, group_w), lambda b, g, i: (0, g)),
        ],
        out_specs=pl.BlockSpec((Q_TILE, group_w), lambda b, g, i: (b * nq + i, g)),
        out_shape=jax.ShapeDtypeStruct((batch * seq, NSA_W), BF16),
        scratch_shapes=[pltpu.VMEM((2, QH_ROWS, LANES), F32),
                        pltpu.VMEM((2, QH_ROWS, LANES), F32),
                        pltpu.VMEM((2, QH_ROWS, NSA_DH), F32)],
        compiler_params=_cparams(("parallel", "parallel", "arbitrary")),
        name="sparse_attention",
    )(proj, cmp_kv, cmp_kv, proj, proj, proj, proj, proj, cmp_bias, tile_bias, overlap, out_g)


def _outproj_kernel(x_ref, ret_ref, nsa_ref, wr_ref, wn_ref, g_ref, h_ref, xn_ref):
    h = (x_ref[...] + jnp.dot(ret_ref[...], wr_ref[...], preferred_element_type=F32)
         + jnp.dot(nsa_ref[...], wn_ref[...], preferred_element_type=F32))
    h_ref[...] = h
    ms = jnp.mean(h * h, axis=-1, keepdims=True)
    xn_ref[...] = h * lax.rsqrt(ms + EPS) * g_ref[...]


def _out_projection(x2, ret, nsa, w_ret, w_nsa, g2, tm=256):
    n = x2.shape[0]
    tm = min(tm, n)
    row = lambda w: pl.BlockSpec((tm, w), lambda i: (i, 0))
    whole = lambda a: pl.BlockSpec(a.shape, lambda i: (0, 0))
    return pl.pallas_call(
        _outproj_kernel,
        grid=(n // tm,),
        in_specs=[row(D_MODEL), row(RET_W), row(NSA_W), whole(w_ret), whole(w_nsa), whole(g2)],
        out_specs=[row(D_MODEL), row(D_MODEL)],
        out_shape=[jax.ShapeDtypeStruct((n, D_MODEL), F32), jax.ShapeDtypeStruct((n, D_MODEL), F32)],
        compiler_params=_cparams(("parallel",)),
        name="out_projection",
    )(x2, ret, nsa, w_ret, w_nsa, g2)


def _top_k_rows(vals, k, payload=None):
    rows = lax.broadcasted_iota(jnp.int32, vals.shape, 0)
    big = vals.shape[0]
    out_row = lax.broadcasted_iota(jnp.int32, (k, vals.shape[1]), 0)
    out_v = jnp.zeros((k, vals.shape[1]), F32)
    out_i = jnp.zeros((k, vals.shape[1]), jnp.int32)
    for r in range(k):
        m = jnp.max(vals, axis=0, keepdims=True)
        pos = jnp.min(jnp.where(vals == m, rows, big), axis=0, keepdims=True)
        hit = rows == pos
        picked = pos if payload is None else jnp.sum(jnp.where(hit, payload, 0), axis=0, keepdims=True)
        out_v = jnp.where(out_row == r, m, out_v)
        out_i = jnp.where(out_row == r, picked, out_i)
        vals = jnp.where(hit, -jnp.inf, vals)
    return out_v, out_i


def _route_kernel(xn_ref, wq_ref, sk_ref, idx_ref, gate_ref):
    pq = jnp.dot(xn_ref[...].astype(BF16), wq_ref[...], preferred_element_type=F32)
    half = PEER_DQ // 2
    for h in range(PEER_HEADS):
        sv, si = [], []
        for p in range(2):
            c0 = (2 * h + p) * half
            scores = _nt_dot(sk_ref[p], pq[:, c0:c0 + half])
            v, i = _top_k_rows(scores, PEER_TOPK)
            sv.append(v)
            si.append(i)
        cand = jnp.concatenate([sv[0][a:a + 1, :] + sv[1] for a in range(PEER_TOPK)], axis=0)
        cidx = jnp.concatenate([si[0][a:a + 1, :] * PEER_NKEYS + si[1] for a in range(PEER_TOPK)], axis=0)
        cs, eidx = _top_k_rows(cand, PEER_TOPK, payload=cidx)
        e = jnp.exp(cs - jnp.max(cs, axis=0, keepdims=True))
        gate_ref[0, h] = e / jnp.sum(e, axis=0, keepdims=True)
        idx_ref[0, h] = eidx


def _peer_route(xn2, wq_bf, subkeys, tm=256):
    n = xn2.shape[0]
    tm = min(tm, n)
    out_block = pl.BlockSpec((1, PEER_HEADS, PEER_TOPK, tm), lambda i: (i, 0, 0, 0))
    shape = (n // tm, PEER_HEADS, PEER_TOPK, tm)
    idx, gate = pl.pallas_call(
        _route_kernel,
        grid=(n // tm,),
        in_specs=[pl.BlockSpec((tm, D_MODEL), lambda i: (i, 0)),
                  pl.BlockSpec(wq_bf.shape, lambda i: (0, 0)),
                  pl.BlockSpec(subkeys.shape, lambda i: (0, 0, 0))],
        out_specs=[out_block, out_block],
        out_shape=[jax.ShapeDtypeStruct(shape, jnp.int32), jax.ShapeDtypeStruct(shape, F32)],
        compiler_params=_cparams(("parallel",)),
        name="peer_route",
    )(xn2, wq_bf, subkeys)
    to_tokens = lambda a: a.transpose(0, 3, 1, 2).reshape(n, PEER_EXPERTS_PER_TOKEN)
    return to_tokens(idx), to_tokens(gate)


SLAB_ROWS = D_MODEL // LANES
PEER_TOKENS_PER_STEP = 128
PEER_RING = 8
PEER_PARTIALS = 4


def _peer_kernel(idx_ref, gate_ref, xs_ref, hs_ref, gf_ref, uv_hbm, o_ref, buf, sem, w_sc, mix):
    n_tok = PEER_TOKENS_PER_STEP
    ept = PEER_EXPERTS_PER_TOKEN
    lane8 = lax.broadcasted_iota(jnp.int32, (SUBLANES, LANES), 1)

    def start_copy(tok, k, slot):
        e = idx_ref[tok * ept + k]
        pltpu.make_async_copy(uv_hbm.at[e], buf.at[slot, k], sem.at[slot]).start()

    def wait_slot(slot):
        pltpu.make_async_copy(uv_hbm.at[pl.ds(0, ept)], buf.at[slot], sem.at[slot]).wait()

    def token(i, prefetch):
        slot = i % PEER_RING
        nxt = i + PEER_RING - 1
        nxt_slot = nxt % PEER_RING
        wait_slot(slot)
        x = xs_ref[i]
        x_lo, x_hi = x[0:SUBLANES], x[SUBLANES:SLAB_ROWS]
        parts = [jnp.zeros((SUBLANES, LANES), F32) for _ in range(PEER_PARTIALS)]
        for k in range(ept):
            if prefetch:
                start_copy(nxt, k, nxt_slot)
            prod = buf[slot, k, 0:SUBLANES, :] * x_lo + buf[slot, k, SUBLANES:SLAB_ROWS, :] * x_hi
            r = jnp.sum(prod, axis=-1, keepdims=True)
            parts[k % PEER_PARTIALS] = jnp.where(lane8 == k, r, parts[k % PEER_PARTIALS])
        hid = jnp.sum(parts[0] + parts[1] + parts[2] + parts[3], axis=0, keepdims=True)
        w_row = jax.nn.gelu(hid) * gate_ref[i]
        w_sc[...] = jnp.transpose(jnp.broadcast_to(w_row, (LANES, LANES)))
        acc = [jnp.zeros((SUBLANES, LANES), F32) for _ in range(2 * PEER_PARTIALS)]
        for k in range(ept):
            wk = jnp.broadcast_to(w_sc[k:k + 1, :], (SUBLANES, LANES))
            j = 2 * (k % PEER_PARTIALS)
            acc[j] = acc[j] + wk * buf[slot, k, SLAB_ROWS:SLAB_ROWS + SUBLANES, :]
            acc[j + 1] = acc[j + 1] + wk * buf[slot, k, SLAB_ROWS + SUBLANES:2 * SLAB_ROWS, :]
        mix[i, 0:SUBLANES, :] = acc[0] + acc[2] + acc[4] + acc[6]
        mix[i, SUBLANES:SLAB_ROWS, :] = acc[1] + acc[3] + acc[5] + acc[7]

    def prologue(j, carry):
        for k in range(ept):
            start_copy(j, k, j)
        return carry

    lax.fori_loop(0, PEER_RING - 1, prologue, 0)
    lax.fori_loop(0, n_tok - (PEER_RING - 1), lambda i, c: (token(i, True), c)[1], 0)
    lax.fori_loop(n_tok - (PEER_RING - 1), n_tok, lambda i, c: (token(i, False), c)[1], 0)

    h2 = hs_ref[...] + mix[...]
    ms = jnp.sum(jnp.sum(h2 * h2, axis=1, keepdims=True), axis=2, keepdims=True) * (1.0 / D_MODEL)
    o_ref[...] = h2 * lax.rsqrt(ms + EPS) * gf_ref[...]


def _peer_experts(idx, gate, xn2, h1, gf, uv):
    n = xn2.shape[0]
    tb = min(PEER_TOKENS_PER_STEP, n)
    assert tb == PEER_TOKENS_PER_STEP and n % tb == 0
    slab = lambda a: a.reshape(n, SLAB_ROWS, LANES)
    tok_block = pl.BlockSpec((tb, SLAB_ROWS, LANES), lambda i: (i, 0, 0))
    out = pl.pallas_call(
        _peer_kernel,
        grid=(n // tb,),
        in_specs=[
            pl.BlockSpec((tb * PEER_EXPERTS_PER_TOKEN,), lambda i: (i,), memory_space=pltpu.SMEM),
            pl.BlockSpec((tb, 1, PEER_EXPERTS_PER_TOKEN), lambda i: (i, 0, 0)),
            tok_block, tok_block,
            pl.BlockSpec((1, SLAB_ROWS, LANES), lambda i: (0, 0, 0)),
            pl.BlockSpec(memory_space=pl.ANY),
        ],
        out_specs=tok_block,
        out_shape=jax.ShapeDtypeStruct((n, SLAB_ROWS, LANES), F32),
        scratch_shapes=[pltpu.VMEM((PEER_RING, PEER_EXPERTS_PER_TOKEN, 2 * SLAB_ROWS, LANES), F32),
                        pltpu.SemaphoreType.DMA((PEER_RING,)),
                        pltpu.VMEM((LANES, LANES), F32),
                        pltpu.VMEM((tb, SLAB_ROWS, LANES), F32)],
        compiler_params=_cparams(("arbitrary",)),
        name="peer_experts",
    )(idx.reshape(-1), gate.reshape(n, 1, PEER_EXPERTS_PER_TOKEN), slab(xn2), slab(h1),
      gf.reshape(1, SLAB_ROWS, LANES), uv)
    return out.reshape(n, D_MODEL)


def kernel(x, norm1_g, w_in, rel_bias, ret_gn_g, cmp_pe, cmp_w1, cmp_w2, nsa_out_g, w_out, norm2_g, peer_wq,
           peer_subkeys, peer_u, peer_v, norm_f_g):
    batch, seq, _ = x.shape
    n = batch * seq
    assert w_in.shape[0] == 1, "the final RMSNorm is fused into the (single) layer's PEER stage"
    x2 = x.reshape(n, D_MODEL)
    proj = _in_projection(x2, norm1_g[0][None, :], _permuted_in_weight(w_in[0]).astype(BF16))
    ret = _retention(proj, ret_gn_g[0][None, :], batch, seq)
    cmp_kv = _compress(proj, cmp_pe[0], cmp_w1[0], cmp_w2[0], batch, seq)
    nsa = _nsa(proj, cmp_kv, rel_bias, nsa_out_g[0][None, :], batch, seq)
    w_o = w_out[0].astype(BF16)
    h1, xn2 = _out_projection(x2, ret, nsa, w_o[:RET_W], w_o[RET_W:], norm2_g[0][None, :])
    idx, gate = _peer_route(xn2, peer_wq[0].astype(BF16), peer_subkeys[0])
    n_exp = peer_u.shape[1]
    uv = jnp.concatenate([peer_u[0].reshape(n_exp, SLAB_ROWS, LANES), peer_v[0].reshape(n_exp, SLAB_ROWS, LANES)],
                         axis=1)
    y = _peer_experts(idx, gate, xn2, h1, norm_f_g, uv)
    return y.reshape(batch, seq, D_MODEL)
```
